```python
import jax, jax.numpy as jnp
from jax import lax
import numpy as np

D_MODEL = 1024
BATCH = 16
SEQ = 4096
DEPTH = 4

D_MIX = 2 * D_MODEL
D_MLSTM = D_MIX // 2
N_MLSTM_HEADS = 4
DV_HEAD = D_MLSTM // N_MLSTM_HEADS
DQK_HEAD = DV_HEAD // 2
D_QK = N_MLSTM_HEADS * DQK_HEAD
D_CONV = D_MIX - D_MLSTM
N_CONV_GROUPS = 16
CONV_WIDTH = 3
CHUNK = 64
EPS = 1e-6

SPLIT_SIZES = (D_QK, D_QK, D_MLSTM, D_MLSTM, D_MLSTM, N_MLSTM_HEADS, N_MLSTM_HEADS,
               D_CONV, D_CONV, D_CONV, D_CONV)
IN_COLS = 2 * D_QK + 3 * D_MLSTM + 2 * N_MLSTM_HEADS + 4 * D_CONV

kernel_name = "hymba_mlstm_shortconv_hybrid"


def _split_points():
    pts = []
    acc = 0
    for s in SPLIT_SIZES[:-1]:
        acc += s
        pts.append(acc)
    return tuple(pts)


def rms_norm(x, g):
    xf = x.astype(jnp.float32)
    y = xf * lax.rsqrt(jnp.mean(xf * xf, axis=-1, keepdims=True) + EPS)
    return (y * g.astype(jnp.float32)).astype(x.dtype)


def group_rms_norm(x, g, n_groups, out_dtype):
    shp = x.shape
    xf = x.astype(jnp.float32).reshape(shp[:-1] + (n_groups, shp[-1] // n_groups))
    xf = xf * lax.rsqrt(jnp.mean(xf * xf, axis=-1, keepdims=True) + EPS)
    return (xf.reshape(shp) * g.astype(jnp.float32)).astype(out_dtype)


def mlstm_chunkwise(q, k, v, log_i, log_f):
    B_, H_, S_, dk = q.shape
    dv = v.shape[-1]
    nc = S_ // CHUNK

    def to_chunks(t):
        t = t.reshape((B_, H_, nc, CHUNK) + t.shape[3:])
        return jnp.moveaxis(t, 2, 0)

    qc = to_chunks(q.astype(jnp.float32))
    kc = to_chunks(k.astype(jnp.float32))
    vc = to_chunks(v.astype(jnp.float32))
    lic = to_chunks(log_i)
    bc = lax.cumsum(to_chunks(log_f), axis=3)
    causal = jnp.tril(jnp.ones((CHUNK, CHUNK), dtype=bool))

    def step(carry, xs):
        C, n, m = carry
        qx, kx, vx, li, b = xs
        D = b[..., :, None] - b[..., None, :] + li[..., None, :]
        D = jnp.where(causal, D, -jnp.inf)
        inter = b + m[..., None]
        m_comb = jnp.maximum(inter, jnp.max(D, axis=-1))
        Dw = jnp.exp(D - m_comb[..., None])
        inter_w = jnp.exp(inter - m_comb)
        s = jnp.einsum('bhtd,bhsd->bhts', qx, kx) * Dw
        num = (jnp.einsum('bhts,bhsv->bhtv', s, vx)
               + inter_w[..., None] * jnp.einsum('bhtd,bhdv->bhtv', qx, C))
        den = jnp.sum(s, axis=-1) + inter_w * jnp.einsum('bhtd,bhd->bht', qx, n)
        h = num / jnp.maximum(jnp.abs(den), jnp.exp(-m_comb))[..., None]
        bL = b[..., -1]
        a = bL[..., None] - b + li
        m_new = jnp.maximum(bL + m, jnp.max(a, axis=-1))
        decay = jnp.exp(bL + m - m_new)
        w = jnp.exp(a - m_new[..., None])
        kw = kx * w[..., None]
        C_new = decay[..., None, None] * C + jnp.einsum('bhsd,bhsv->bhdv', kw, vx)
        n_new = decay[..., None] * n + jnp.sum(kw, axis=2)
        return (C_new, n_new, m_new), h

    init = (jnp.zeros((B_, H_, dk, dv), jnp.float32),
            jnp.zeros((B_, H_, dk), jnp.float32),
            jnp.zeros((B_, H_), jnp.float32))
    _, hs = lax.scan(step, init, (qc, kc, vc, lic, bc))
    return jnp.moveaxis(hs, 0, 2).reshape(B_, H_, S_, dv)


def causal_depthwise_conv(u, w):
    S_ = u.shape[1]
    up = jnp.pad(u, ((0, 0), (CONV_WIDTH - 1, 0), (0, 0)))
    y = w[0] * up[:, 0:S_]
    for j in range(1, CONV_WIDTH):
        y = y + w[j] * up[:, j:j + S_]
    return y


def hybrid_layer(x, g_pre, g_post, w_in, b_i, b_f, g_head, conv_w, g_conv, w_out):
    B_, S_, _ = x.shape
    h = rms_norm(x, g_pre)
    proj = jnp.einsum('bsd,de->bse', h, w_in)
    q, k, v, o, z_m, i_pre, f_pre, u, gate_b, gate_c, z_c = jnp.split(
        proj, _split_points(), axis=-1)

    def heads(t, dh):
        return t.reshape(B_, S_, N_MLSTM_HEADS, dh).transpose(0, 2, 1, 3)

    qh = heads(q, DQK_HEAD) * (DQK_HEAD ** -0.5)
    kh = heads(k, DQK_HEAD)
    vh = heads(v, DV_HEAD)
    log_i = (i_pre.astype(jnp.float32) + b_i.astype(jnp.float32)).transpose(0, 2, 1)
    log_f = jax.nn.log_sigmoid(
        f_pre.astype(jnp.float32) + b_f.astype(jnp.float32)).transpose(0, 2, 1)
    hm = mlstm_chunkwise(qh, kh, vh, log_i, log_f)
    hm = hm.transpose(0, 2, 1, 3).reshape(B_, S_, D_MLSTM)
    hm = group_rms_norm(hm, g_head, N_MLSTM_HEADS, x.dtype)
    y_m = jax.nn.silu(z_m) * jax.nn.sigmoid(o) * hm

    conv = causal_depthwise_conv(gate_c * u, conv_w)
    y_c = jax.nn.silu(z_c) * group_rms_norm(gate_b * conv, g_conv, N_CONV_GROUPS, x.dtype)

    mix = jnp.concatenate([y_m, y_c], axis=-1)
    out = jnp.einsum('bse,ed->bsd', mix, w_out)
    return x + rms_norm(out, g_post)


def setup_inputs(seed: int = 0) -> dict:
    key = jax.random.key(seed)
    ks = jax.random.split(key, 11)
    x = jax.random.normal(ks[0], (BATCH, SEQ, D_MODEL), jnp.float32)
    norm_pre = 1.0 + 0.02 * jax.random.normal(ks[1], (DEPTH, D_MODEL), jnp.float32)
    norm_post = 1.0 + 0.02 * jax.random.normal(ks[2], (DEPTH, D_MODEL), jnp.float32)
    w_in = jax.random.normal(ks[3], (DEPTH, D_MODEL, IN_COLS), jnp.float32) * (D_MODEL ** -0.5)
    b_igate = 0.1 * jax.random.normal(ks[4], (DEPTH, N_MLSTM_HEADS), jnp.float32)
    b_fgate = (jnp.linspace(3.0, 6.0, N_MLSTM_HEADS, dtype=jnp.float32)[None, :]
               + 0.1 * jax.random.normal(ks[5], (DEPTH, N_MLSTM_HEADS), jnp.float32))
    head_norm = 1.0 + 0.02 * jax.random.normal(ks[6], (DEPTH, D_MLSTM), jnp.float32)
    conv_w = jax.random.normal(ks[7], (DEPTH, CONV_WIDTH, D_CONV), jnp.float32) * (CONV_WIDTH ** -0.5)
    conv_norm = 1.0 + 0.02 * jax.random.normal(ks[8], (DEPTH, D_CONV), jnp.float32)
    w_out = jax.random.normal(ks[9], (DEPTH, D_MIX, D_MODEL), jnp.float32) * (D_MIX ** -0.5)
    return {"x": x, "norm_pre": norm_pre, "norm_post": norm_post, "w_in": w_in,
            "b_igate": b_igate, "b_fgate": b_fgate, "head_norm": head_norm,
            "conv_w": conv_w, "conv_norm": conv_norm, "w_out": w_out}


def reference(x, norm_pre, norm_post, w_in, b_igate, b_fgate, head_norm, conv_w, conv_norm, w_out):
    for l in range(DEPTH):
        x = hybrid_layer(x, norm_pre[l], norm_post[l], w_in[l], b_igate[l], b_fgate[l],
                         head_norm[l], conv_w[l], conv_norm[l], w_out[l])
    return x
```

```python
import functools

import jax
import jax.numpy as jnp
from jax import lax
from jax.experimental import pallas as pl
from jax.experimental.pallas import tpu as pltpu

N_HEADS = 4
DQK = 128
DV = 256
D_QK = N_HEADS * DQK
D_ML = N_HEADS * DV
D_CV = 1024
N_GROUPS = 16
CONV_W = 3
EPS = 1e-6
LANES = 128
CHUNK = LANES
DV_EXT = DV + LANES
SEQ_TILE = 512
VMEM_LIMIT = 56 * 1024 * 1024

_Q0, _K0, _V0, _O0, _ZM0, _U0, _B0, _C0, _ZC0 = (
    0, 512, 1024, 2048, 3072, 4096, 5120, 6144, 7168)


def _dot(a, b):
    return jnp.dot(a, b, preferred_element_type=jnp.float32)


def _sigmoid(x):
    return 1.0 / (1.0 + jnp.exp(-x))


def _rms(x, g):
    ms = jnp.mean(x * x, axis=-1, keepdims=True)
    return x * lax.rsqrt(ms + EPS) * g


def _chunk_cumsum(x):
    lane = lax.broadcasted_iota(jnp.int32, x.shape, 1)
    k = 1
    while k < LANES:
        x = x + jnp.where(lane >= k, pltpu.roll(x, k, axis=1), 0.0)
        k *= 2
    return x


def _layer_kernel(x_ref, gpre_ref, gpost_ref, wmain_ref, wif_ref, bif_ref, ghead_ref,
                  convw_ref, gconv_ref, wout_ref, rmat_ref, emat_ref,
                  o_ref, c_ref, m_ref, cu_ref, mix_ref, *, ts):
    nchunk = ts // CHUNK
    j = pl.program_id(1)

    @pl.when(j == 0)
    def _():
        c_ref[...] = jnp.zeros_like(c_ref)
        m_ref[...] = jnp.zeros_like(m_ref)
        cu_ref[0:8, :] = jnp.zeros((8, D_CV), jnp.float32)

    x = x_ref[0]
    h = _rms(x, gpre_ref[...]).astype(jnp.bfloat16)

    g = _dot(h, wif_ref[...]) + bif_ref[...]
    lane = lax.broadcasted_iota(jnp.int32, g.shape, 1)
    log_sig = jnp.minimum(g, 0.0) - jnp.log1p(jnp.exp(-jnp.abs(g)))
    g = jnp.where(lane >= N_HEADS, log_sig, g)
    g_row = g.T[0:8, :]
    p_rows = []
    for c in range(nchunk):
        blk = g_row[:, c * CHUNK:(c + 1) * CHUNK]
        cs = _chunk_cumsum(blk)
        r = blk - pltpu.roll(cs, N_HEADS, axis=0)
        sub = lax.broadcasted_iota(jnp.int32, blk.shape, 0)
        p_rows.append(jnp.where(sub < N_HEADS, r, cs))
    p_row = jnp.concatenate(p_rows, axis=1)
    p_col = jnp.concatenate(
        [p_row, jnp.zeros((LANES - 8, ts), jnp.float32)], axis=0).T

    qkv = _dot(h, wmain_ref[:, _Q0:_O0])
    oz = _dot(h, wmain_ref[:, _O0:_U0])
    tri = (lax.broadcasted_iota(jnp.int32, (CHUNK, CHUNK), 0)
           >= lax.broadcasted_iota(jnp.int32, (CHUNK, CHUNK), 1))
    ones_blk = jnp.ones((CHUNK, LANES), jnp.bfloat16)
    for hd in range(N_HEADS):
        cmat = c_ref[hd]
        m = m_ref[hd][0:1, 0:1]
        for c in range(nchunk):
            rows = slice(c * CHUNK, (c + 1) * CHUNK)
            q_c = (qkv[rows, _Q0 + hd * DQK:_Q0 + (hd + 1) * DQK] * (DQK ** -0.5)
                   ).astype(jnp.bfloat16)
            k_f = qkv[rows, _K0 + hd * DQK:_K0 + (hd + 1) * DQK]
            v_c = qkv[rows, _V0 + hd * DV:_V0 + (hd + 1) * DV].astype(jnp.bfloat16)
            v_ext = jnp.concatenate([v_c, ones_blk], axis=1)
            b_col = p_col[rows, N_HEADS + hd:N_HEADS + hd + 1]
            r_col = p_col[rows, hd:hd + 1]
            r_row = p_row[hd:hd + 1, rows]
            b_last = p_row[N_HEADS + hd:N_HEADS + hd + 1,
                           (c + 1) * CHUNK - 1:(c + 1) * CHUNK]

            d = jnp.where(tri, b_col + r_row, -jnp.inf)
            inter = b_col + m
            m_comb = jnp.maximum(inter, jnp.max(d, axis=1, keepdims=True))
            dw = jnp.exp(d - m_comb)
            inter_w = jnp.exp(inter - m_comb)
            s = lax.dot_general(q_c, k_f.astype(jnp.bfloat16),
                                (((1,), (1,)), ((), ())),
                                preferred_element_type=jnp.float32)
            sw = (s * dw).astype(jnp.bfloat16)
            tot = _dot(sw, v_ext) + inter_w * _dot(q_c, cmat.astype(jnp.bfloat16))
            den = jnp.maximum(jnp.abs(tot[:, DV:]), jnp.exp(-m_comb))
            inv = 1.0 / den
            hout = tot[:, :DV] * jnp.concatenate([inv, inv], axis=1)

            mx = jnp.maximum(m, jnp.max(r_row, axis=1, keepdims=True))
            decay = jnp.exp(m - mx)
            kw = (k_f * jnp.exp(r_col - mx)).astype(jnp.bfloat16)
            upd = lax.dot_general(kw, v_ext, (((0,), (0,)), ((), ())),
                                  preferred_element_type=jnp.float32)
            cmat = decay * cmat + upd
            m = b_last + mx

            hn = hout * lax.rsqrt(jnp.mean(hout * hout, axis=-1, keepdims=True) + EPS)
            hn = hn * ghead_ref[:, hd * DV:(hd + 1) * DV]
            o_c = oz[rows, hd * DV:(hd + 1) * DV]
            z_c = oz[rows, D_ML + hd * DV:D_ML + (hd + 1) * DV]
            y_m = z_c * _sigmoid(z_c) * _sigmoid(o_c) * hn
            mix_ref[rows, hd * DV:(hd + 1) * DV] = y_m.astype(jnp.bfloat16)
        c_ref[hd] = cmat
        m_ref[hd] = jnp.broadcast_to(m, (8, LANES))

    ubc = _dot(h, wmain_ref[:, _U0:_ZC0])
    cu_ref[8:8 + ts, :] = ubc[:, 2 * D_CV:] * ubc[:, :D_CV]
    conv = (convw_ref[0:1, :] * cu_ref[6:6 + ts, :]
            + convw_ref[1:2, :] * cu_ref[7:7 + ts, :]
            + convw_ref[2:3, :] * cu_ref[8:8 + ts, :])
    cu_ref[0:8, :] = cu_ref[ts:ts + 8, :]
    t = ubc[:, D_CV:2 * D_CV] * conv
    ms = _dot((t * t).astype(jnp.bfloat16), rmat_ref[...])
    rs = lax.rsqrt(ms + EPS)
    rs_hi = rs.astype(jnp.bfloat16)
    rs_lo = (rs - rs_hi.astype(jnp.float32)).astype(jnp.bfloat16)
    scale = _dot(rs_hi, emat_ref[...]) + _dot(rs_lo, emat_ref[...])
    zc = _dot(h, wmain_ref[:, _ZC0:])
    y_c = zc * _sigmoid(zc) * (t * scale * gconv_ref[...])
    mix_ref[:, D_ML:] = y_c.astype(jnp.bfloat16)

    out = _dot(mix_ref[...], wout_ref[...])
    o_ref[0] = x + _rms(out, gpost_ref[...])


def _full(shape):
    return pl.BlockSpec(shape, lambda b, j: (0,) * len(shape))


def _layer(x, g_pre, g_post, w_in, b_i, b_f, g_head, conv_w, g_conv, w_out, *, ts):
    bsz, seq, d = x.shape
    assert seq % ts == 0 and ts % CHUNK == 0
    gate0 = _U0
    w_main = jnp.concatenate([w_in[:, :gate0], w_in[:, gate0 + 2 * N_HEADS:]],
                             axis=1).astype(jnp.bfloat16)
    w_if = jnp.pad(w_in[:, gate0:gate0 + 2 * N_HEADS],
                   ((0, 0), (0, LANES - 2 * N_HEADS))).astype(jnp.bfloat16)
    b_if = jnp.pad(jnp.concatenate([b_i, b_f]), (0, LANES - 2 * N_HEADS))[None, :]
    grp = jnp.arange(D_CV) // (D_CV // N_GROUPS)
    rmat = (grp[:, None] == jnp.arange(LANES)[None, :]).astype(jnp.float32)
    emat = rmat.T.astype(jnp.bfloat16)
    rmat = (rmat / (D_CV // N_GROUPS)).astype(jnp.bfloat16)
    n_in = w_main.shape[1]

    kern = functools.partial(_layer_kernel, ts=ts)
    return pl.pallas_call(
        kern,
        grid=(bsz, seq // ts),
        in_specs=[
            pl.BlockSpec((1, ts, d), lambda b, j: (b, j, 0)),
            _full((1, d)), _full((1, d)),
            _full((d, n_in)), _full((d, LANES)), _full((1, LANES)),
            _full((1, D_ML)), _full((CONV_W, D_CV)), _full((1, D_CV)),
            _full((D_ML + D_CV, d)), _full((D_CV, LANES)), _full((LANES, D_CV)),
        ],
        out_specs=pl.BlockSpec((1, ts, d), lambda b, j: (b, j, 0)),
        out_shape=jax.ShapeDtypeStruct(x.shape, x.dtype),
        scratch_shapes=[
            pltpu.VMEM((N_HEADS, DQK, DV_EXT), jnp.float32),
            pltpu.VMEM((N_HEADS, 8, LANES), jnp.float32),
            pltpu.VMEM((ts + 8, D_CV), jnp.float32),
            pltpu.VMEM((ts, D_ML + D_CV), jnp.bfloat16),
        ],
        compiler_params=pltpu.CompilerParams(
            dimension_semantics=("parallel", "arbitrary"),
            vmem_limit_bytes=VMEM_LIMIT),
        name="hybrid_layer",
    )(x, g_pre[None, :], g_post[None, :], w_main, w_if, b_if, g_head[None, :],
      conv_w, g_conv[None, :], w_out.astype(jnp.bfloat16), rmat, emat)


def kernel(x, norm_pre, norm_post, w_in, b_igate, b_fgate, head_norm, conv_w, conv_norm, w_out):
    ts = min(SEQ_TILE, x.shape[1])
    for l in range(norm_pre.shape[0]):
        x = _layer(x, norm_pre[l], norm_post[l], w_in[l], b_igate[l], b_fgate[l],
                   head_norm[l], conv_w[l], conv_norm[l], w_out[l], ts=ts)
    return x
```

```python
import functools

import jax
import jax.numpy as jnp
from jax import lax
from jax.experimental import pallas as pl
from jax.experimental.pallas import tpu as pltpu

N_HEADS = 4
DQK = 128
DV = 256
D_ML = N_HEADS * DV
D_CV = 1024
N_GROUPS = 16
CONV_W = 3
EPS = 1e-6
LANES = 128
MXU_TILE = 256
CHUNK = LANES
DV_EXT = DV + LANES
SEQ_TILE = 512
VMEM_LIMIT = 58 * 1024 * 1024

_Q0, _K0, _V0, _O0, _ZM0, _U0 = 0, 512, 1024, 2048, 3072, 4096
_CV_U, _CV_B, _CV_C, _CV_Z = 0, D_CV, 2 * D_CV, 3 * D_CV


def _dot(a, b):
    return jnp.dot(a, b, preferred_element_type=jnp.float32)


def _sigmoid(x):
    return 1.0 / (1.0 + jnp.exp(-x))


def _rms(x, g):
    ms = jnp.mean(x * x, axis=-1, keepdims=True)
    return x * lax.rsqrt(ms + EPS) * g


def _chunk_cumsum(x):
    lane = lax.broadcasted_iota(jnp.int32, x.shape, 1)
    k = 1
    while k < LANES:
        x = x + jnp.where(lane >= k, pltpu.roll(x, k, axis=1), 0.0)
        k *= 2
    return x


def _conv_block(k, cv_ref, cu_ref, mix_ref, convw_ref, gconv_ref, rmat_ref, emat2_ref, ts):
    cols = slice(k * MXU_TILE, (k + 1) * MXU_TILE)

    def cv(off):
        return cv_ref[:, off + k * MXU_TILE:off + (k + 1) * MXU_TILE]

    cu_ref[8:8 + ts, cols] = cv(_CV_C) * cv(_CV_U)
    conv = (convw_ref[0:1, cols] * cu_ref[6:6 + ts, cols]
            + convw_ref[1:2, cols] * cu_ref[7:7 + ts, cols]
            + convw_ref[2:3, cols] * cu_ref[8:8 + ts, cols])
    cu_ref[0:8, cols] = cu_ref[ts:ts + 8, cols]
    t = cv(_CV_B) * conv
    ms = _dot((t * t).astype(jnp.bfloat16), rmat_ref[cols, :])
    rs = lax.rsqrt(ms + EPS)
    rs_hi = rs.astype(jnp.bfloat16)
    rs_lo = (rs - rs_hi.astype(jnp.float32)).astype(jnp.bfloat16)
    scale = _dot(jnp.concatenate([rs_hi, rs_lo], axis=1), emat2_ref[:, cols])
    zc = cv(_CV_Z)
    y_c = zc * _sigmoid(zc) * (t * scale * gconv_ref[:, cols])
    mix_ref[:, D_ML + k * MXU_TILE:D_ML + (k + 1) * MXU_TILE] = y_c.astype(jnp.bfloat16)


def _prenorm_and_qkv(x_tile_ref, gpre_ref, wmain_ref, h_ref, qkv_ref):
    h_ref[...] = _rms(x_tile_ref[0], gpre_ref[...]).astype(jnp.bfloat16)
    qkv_ref[...] = _dot(h_ref[...], wmain_ref[:, _Q0:_O0])


def _layer_kernel(x_ref, xn_ref, gpre_ref, gpost_ref, wmain_ref, wif_ref, bif_ref,
                  ghead_ref, convw_ref, gconv_ref, wout_ref, rmat_ref, emat2_ref,
                  o_ref, c_ref, m_ref, cu_ref, mix_ref, cv_ref, h_ref, qkv_ref,
                  *, ts, tiles_per_seq):
    nchunk = ts // CHUNK
    s = pl.program_id(0)

    @pl.when(s == 0)
    def _():
        _prenorm_and_qkv(x_ref, gpre_ref, wmain_ref, h_ref, qkv_ref)

    @pl.when(s % tiles_per_seq == 0)
    def _():
        c_ref[...] = jnp.zeros_like(c_ref)
        m_ref[...] = jnp.zeros_like(m_ref)
        cu_ref[0:8, :] = jnp.zeros((8, D_CV), jnp.float32)

    g = _dot(h_ref[...], wif_ref[...]) + bif_ref[...]
    lane = lax.broadcasted_iota(jnp.int32, g.shape, 1)
    log_sig = jnp.minimum(g, 0.0) - jnp.log1p(jnp.exp(-jnp.abs(g)))
    g = jnp.where(lane >= N_HEADS, log_sig, g)
    g_row = g.T[0:8, :]
    p_rows = []
    for c in range(nchunk):
        blk = g_row[:, c * CHUNK:(c + 1) * CHUNK]
        cs = _chunk_cumsum(blk)
        r = blk - pltpu.roll(cs, N_HEADS, axis=0)
        sub = lax.broadcasted_iota(jnp.int32, blk.shape, 0)
        p_rows.append(jnp.where(sub < N_HEADS, r, cs))
    p_row = jnp.concatenate(p_rows, axis=1)
    p_col = jnp.concatenate(
        [p_row, jnp.zeros((LANES - 8, ts), jnp.float32)], axis=0).T

    oz = _dot(h_ref[...], wmain_ref[:, _O0:_U0])
    tri = (lax.broadcasted_iota(jnp.int32, (CHUNK, CHUNK), 0)
           >= lax.broadcasted_iota(jnp.int32, (CHUNK, CHUNK), 1))
    ones_blk = jnp.ones((CHUNK, LANES), jnp.bfloat16)
    n_iter = N_HEADS * nchunk
    n_blk = D_CV // MXU_TILE
    pieces = [(off, k) for k in range(n_blk) for off in (_CV_U, _CV_C, _CV_B, _CV_Z)]
    for hd in range(N_HEADS):
        cmat = c_ref[hd]
        m = m_ref[hd][0:1, 0:1]
        for c in range(nchunk):
            it = hd * nchunk + c
            for off, k in pieces[it * len(pieces) // n_iter:(it + 1) * len(pieces) // n_iter]:
                lo = off + k * MXU_TILE
                cv_ref[:, lo:lo + MXU_TILE] = _dot(
                    h_ref[...], wmain_ref[:, _U0 + lo:_U0 + lo + MXU_TILE])
                if off == _CV_Z:
                    _conv_block(k, cv_ref, cu_ref, mix_ref, convw_ref, gconv_ref,
                                rmat_ref, emat2_ref, ts)
            rows = slice(c * CHUNK, (c + 1) * CHUNK)
            q_f = qkv_ref[rows,_Q0 + hd * DQK:_Q0 + (hd + 1) * DQK] * (DQK ** -0.5)
            q_c = q_f.astype(jnp.bfloat16)
            k_f = qkv_ref[rows,_K0 + hd * DQK:_K0 + (hd + 1) * DQK]
            v_c = qkv_ref[rows,_V0 + hd * DV:_V0 + (hd + 1) * DV].astype(jnp.bfloat16)
            v_ext = jnp.concatenate([v_c, ones_blk], axis=1)
            b_col = p_col[rows, N_HEADS + hd:N_HEADS + hd + 1]
            r_col = p_col[rows, hd:hd + 1]
            r_row = p_row[hd:hd + 1, rows]
            b_last = p_row[N_HEADS + hd:N_HEADS + hd + 1,
                           (c + 1) * CHUNK - 1:(c + 1) * CHUNK]

            d = jnp.where(tri, b_col + r_row, -jnp.inf)
            inter = b_col + m
            m_comb = jnp.maximum(inter, jnp.max(d, axis=1, keepdims=True))
            dw = jnp.exp(d - m_comb)
            inter_w = jnp.exp(inter - m_comb)
            s = lax.dot_general(q_c, k_f.astype(jnp.bfloat16),
                                (((1,), (1,)), ((), ())),
                                preferred_element_type=jnp.float32)
            sw = (s * dw).astype(jnp.bfloat16)
            qs = (q_f * inter_w).astype(jnp.bfloat16)
            tot = _dot(jnp.concatenate([sw, qs], axis=1),
                       jnp.concatenate([v_ext, cmat.astype(jnp.bfloat16)], axis=0))
            den = jnp.maximum(jnp.abs(tot[:, DV:]), jnp.exp(-m_comb))
            inv = 1.0 / den
            hout = tot[:, :DV] * jnp.concatenate([inv, inv], axis=1)

            mx = jnp.maximum(m, jnp.max(r_row, axis=1, keepdims=True))
            decay = jnp.exp(m - mx)
            kw = (k_f * jnp.exp(r_col - mx)).astype(jnp.bfloat16)
            upd = lax.dot_general(kw, v_ext, (((0,), (0,)), ((), ())),
                                  preferred_element_type=jnp.float32)
            cmat = decay * cmat + upd
            m = b_last + mx

            hn = hout * lax.rsqrt(jnp.mean(hout * hout, axis=-1, keepdims=True) + EPS)
            hn = hn * ghead_ref[:, hd * DV:(hd + 1) * DV]
            o_c = oz[rows, hd * DV:(hd + 1) * DV]
            z_c = oz[rows, D_ML + hd * DV:D_ML + (hd + 1) * DV]
            y_m = z_c * _sigmoid(z_c) * _sigmoid(o_c) * hn
            mix_ref[rows, hd * DV:(hd + 1) * DV] = y_m.astype(jnp.bfloat16)
        c_ref[hd] = cmat
        m_ref[hd] = jnp.broadcast_to(m, (8, LANES))

    out = _dot(mix_ref[...], wout_ref[...])
    _prenorm_and_qkv(xn_ref, gpre_ref, wmain_ref, h_ref, qkv_ref)
    o_ref[0] = x_ref[0] + _rms(out, gpost_ref[...])


def _full(shape):
    return pl.BlockSpec(shape, lambda s: (0,) * len(shape))


def _layer(x, g_pre, g_post, w_in, b_i, b_f, g_head, conv_w, g_conv, w_out, *, ts):
    bsz, seq, d = x.shape
    assert seq % ts == 0 and ts % CHUNK == 0
    tiles_per_seq = seq // ts
    n_tiles = bsz * tiles_per_seq
    gate0 = _U0
    w_main = jnp.concatenate([w_in[:, :gate0], w_in[:, gate0 + 2 * N_HEADS:]],
                             axis=1).astype(jnp.bfloat16)
    w_if = jnp.pad(w_in[:, gate0:gate0 + 2 * N_HEADS],
                   ((0, 0), (0, LANES - 2 * N_HEADS))).astype(jnp.bfloat16)
    b_if = jnp.pad(jnp.concatenate([b_i, b_f]), (0, LANES - 2 * N_HEADS))[None, :]
    grp = jnp.arange(D_CV) // (D_CV // N_GROUPS)
    rmat = (grp[:, None] == jnp.arange(LANES)[None, :]).astype(jnp.float32)
    emat = rmat.T.astype(jnp.bfloat16)
    emat2 = jnp.concatenate([emat, emat], axis=0)
    rmat = (rmat / (D_CV // N_GROUPS)).astype(jnp.bfloat16)
    n_in = w_main.shape[1]
    xt = x.reshape(n_tiles, ts, d)

    kern = functools.partial(_layer_kernel, ts=ts, tiles_per_seq=tiles_per_seq)
    out = pl.pallas_call(
        kern,
        grid=(n_tiles,),
        in_specs=[
            pl.BlockSpec((1, ts, d), lambda s: (s, 0, 0)),
            pl.BlockSpec((1, ts, d), lambda s: (jnp.minimum(s + 1, n_tiles - 1), 0, 0)),
            _full((1, d)), _full((1, d)),
            _full((d, n_in)), _full((d, LANES)), _full((1, LANES)),
            _full((1, D_ML)), _full((CONV_W, D_CV)), _full((1, D_CV)),
            _full((D_ML + D_CV, d)), _full((D_CV, LANES)), _full((2 * LANES, D_CV)),
        ],
        out_specs=pl.BlockSpec((1, ts, d), lambda s: (s, 0, 0)),
        out_shape=jax.ShapeDtypeStruct(xt.shape, x.dtype),
        scratch_shapes=[
            pltpu.VMEM((N_HEADS, DQK, DV_EXT), jnp.float32),
            pltpu.VMEM((N_HEADS, 8, LANES), jnp.float32),
            pltpu.VMEM((ts + 8, D_CV), jnp.float32),
            pltpu.VMEM((ts, D_ML + D_CV), jnp.bfloat16),
            pltpu.VMEM((ts, 4 * D_CV), jnp.float32),
            pltpu.VMEM((ts, d), jnp.bfloat16),
            pltpu.VMEM((ts, 2 * DQK * N_HEADS + D_ML), jnp.float32),
        ],
        compiler_params=pltpu.CompilerParams(
            dimension_semantics=("arbitrary",),
            vmem_limit_bytes=VMEM_LIMIT),
        name="hybrid_layer",
    )(xt, xt, g_pre[None, :], g_post[None, :], w_main, w_if, b_if, g_head[None, :],
      conv_w, g_conv[None, :], w_out.astype(jnp.bfloat16), rmat, emat2)
    return out.reshape(x.shape)


def kernel(x, norm_pre, norm_post, w_in, b_igate, b_fgate, head_norm, conv_w, conv_norm, w_out):
    ts = min(SEQ_TILE, x.shape[1])
    for l in range(norm_pre.shape[0]):
        x = _layer(x, norm_pre[l], norm_post[l], w_in[l], b_igate[l], b_fgate[l],
                   head_norm[l], conv_w[l], conv_norm[l], w_out[l], ts=ts)
    return x
```

```python
import functools

import jax
import jax.numpy as jnp
from jax import lax
from jax.experimental import pallas as pl
from jax.experimental.pallas import tpu as pltpu

N_HEADS = 4
DQK = 128
DV = 256
D_ML = N_HEADS * DV
D_CV = 1024
N_GROUPS = 16
CONV_W = 3
EPS = 1e-6
LANES = 128
MXU_TILE = 256
CHUNK = LANES
DV_EXT = DV + LANES
SEQ_TILE = 512
VMEM_LIMIT = 58 * 1024 * 1024

_Q0, _K0, _V0, _O0, _ZM0, _U0 = 0, 512, 1024, 2048, 3072, 4096
_CV_U, _CV_B, _CV_C, _CV_Z = 0, D_CV, 2 * D_CV, 3 * D_CV


def _dot(a, b):
    return jnp.dot(a, b, preferred_element_type=jnp.float32)


def _sigmoid(x):
    return 1.0 / (1.0 + jnp.exp(-x))


def _rms(x, g):
    ms = jnp.mean(x * x, axis=-1, keepdims=True)
    return x * lax.rsqrt(ms + EPS) * g


def _chunk_cumsum(x):
    lane = lax.broadcasted_iota(jnp.int32, x.shape, 1)
    k = 1
    while k < LANES:
        x = x + jnp.where(lane >= k, pltpu.roll(x, k, axis=1), 0.0)
        k *= 2
    return x


def _conv_block(k, cv_ref, cu_ref, mix_ref, convw_ref, gconv_ref, rmat_ref, emat2_ref, ts):
    cols = slice(k * MXU_TILE, (k + 1) * MXU_TILE)

    def cv(off):
        return cv_ref[:, off + k * MXU_TILE:off + (k + 1) * MXU_TILE]

    cu_ref[8:8 + ts, cols] = cv(_CV_C) * cv(_CV_U)
    conv = (convw_ref[0:1, cols] * cu_ref[6:6 + ts, cols]
            + convw_ref[1:2, cols] * cu_ref[7:7 + ts, cols]
            + convw_ref[2:3, cols] * cu_ref[8:8 + ts, cols])
    cu_ref[0:8, cols] = cu_ref[ts:ts + 8, cols]
    t = cv(_CV_B) * conv
    ms = _dot((t * t).astype(jnp.bfloat16), rmat_ref[cols, :])
    rs = lax.rsqrt(ms + EPS)
    rs_hi = rs.astype(jnp.bfloat16)
    rs_lo = (rs - rs_hi.astype(jnp.float32)).astype(jnp.bfloat16)
    scale = _dot(jnp.concatenate([rs_hi, rs_lo], axis=1), emat2_ref[:, cols])
    zc = cv(_CV_Z)
    y_c = zc * _sigmoid(zc) * (t * scale * gconv_ref[:, cols])
    mix_ref[:, D_ML + k * MXU_TILE:D_ML + (k + 1) * MXU_TILE] = y_c.astype(jnp.bfloat16)


def _prenorm_and_qkv(xn_ref, gpre_ref, wmain_ref, h_ref, qkv_ref):
    h_ref[...] = _rms(xn_ref[0], gpre_ref[...]).astype(jnp.bfloat16)
    qkv_ref[...] = _dot(h_ref[...], wmain_ref[:, _Q0:_O0])


def _layer_kernel(xn_ref, gpre_ref, gpost_ref, wmain_ref, wif_ref, bif_ref,
                  ghead_ref, convw_ref, gconv_ref, wout_ref, rmat_ref, emat2_ref,
                  o_ref, c_ref, m_ref, cu_ref, mix_ref, cv_ref, oz_ref, h_ref, qkv_ref, xk_ref,
                  *, ts, tiles_per_seq):
    s = pl.program_id(0)

    @pl.when(s == 0)
    def _():
        _prenorm_and_qkv(xn_ref, gpre_ref, wmain_ref, h_ref, qkv_ref)
        xk_ref[...] = xn_ref[0]

    @pl.when((s - 1) % tiles_per_seq == 0)
    def _():
        c_ref[...] = jnp.zeros_like(c_ref)
        m_ref[...] = jnp.zeros_like(m_ref)
        cu_ref[0:8, :] = jnp.zeros((8, D_CV), jnp.float32)

    @pl.when(s > 0)
    def _():
        _tile_body(xn_ref, gpre_ref, gpost_ref, wmain_ref, wif_ref, bif_ref, ghead_ref,
                   convw_ref, gconv_ref, wout_ref, rmat_ref, emat2_ref, o_ref, c_ref, m_ref,
                   cu_ref, mix_ref, cv_ref, oz_ref, h_ref, qkv_ref, xk_ref, ts)


def _tile_body(xn_ref, gpre_ref, gpost_ref, wmain_ref, wif_ref, bif_ref, ghead_ref,
               convw_ref, gconv_ref, wout_ref, rmat_ref, emat2_ref, o_ref, c_ref, m_ref,
               cu_ref, mix_ref, cv_ref, oz_ref, h_ref, qkv_ref, xk_ref, ts):
    nchunk = ts // CHUNK
    assert nchunk >= 2

    g = _dot(h_ref[...], wif_ref[...]) + bif_ref[...]
    lane = lax.broadcasted_iota(jnp.int32, g.shape, 1)
    log_sig = jnp.minimum(g, 0.0) - jnp.log1p(jnp.exp(-jnp.abs(g)))
    g = jnp.where(lane >= N_HEADS, log_sig, g)
    g_row = g.T[0:8, :]
    p_rows = []
    for c in range(nchunk):
        blk = g_row[:, c * CHUNK:(c + 1) * CHUNK]
        cs = _chunk_cumsum(blk)
        r = blk - pltpu.roll(cs, N_HEADS, axis=0)
        sub = lax.broadcasted_iota(jnp.int32, blk.shape, 0)
        p_rows.append(jnp.where(sub < N_HEADS, r, cs))
    p_row = jnp.concatenate(p_rows, axis=1)
    p_col = jnp.concatenate(
        [p_row, jnp.zeros((LANES - 8, ts), jnp.float32)], axis=0).T

    tri = (lax.broadcasted_iota(jnp.int32, (CHUNK, CHUNK), 0)
           >= lax.broadcasted_iota(jnp.int32, (CHUNK, CHUNK), 1))
    ones_blk = jnp.ones((CHUNK, LANES), jnp.bfloat16)

    def project(dst_ref, dst_lo, src_lo):
        dst_ref[:, dst_lo:dst_lo + MXU_TILE] = _dot(
            h_ref[...], wmain_ref[:, src_lo:src_lo + MXU_TILE])

    oz_ref[...] = _dot(h_ref[...], wmain_ref[:, _O0:_U0])
    n_iter = N_HEADS * nchunk
    n_blk = D_CV // MXU_TILE
    pieces = [(off, k) for k in range(n_blk) for off in (_CV_U, _CV_C, _CV_B, _CV_Z)]
    for hd in range(N_HEADS):
        cmat = c_ref[hd]
        m = m_ref[hd][0:1, 0:1]
        for c in range(nchunk):
            it = hd * nchunk + c
            for off, k in pieces[it * len(pieces) // n_iter:(it + 1) * len(pieces) // n_iter]:
                project(cv_ref, off + k * MXU_TILE, _U0 + off + k * MXU_TILE)
                if off == _CV_Z:
                    _conv_block(k, cv_ref, cu_ref, mix_ref, convw_ref, gconv_ref,
                                rmat_ref, emat2_ref, ts)
            rows = slice(c * CHUNK, (c + 1) * CHUNK)
            q_f = qkv_ref[rows, _Q0 + hd * DQK:_Q0 + (hd + 1) * DQK] * (DQK ** -0.5)
            q_c = q_f.astype(jnp.bfloat16)
            k_f = qkv_ref[rows, _K0 + hd * DQK:_K0 + (hd + 1) * DQK]
            v_c = qkv_ref[rows, _V0 + hd * DV:_V0 + (hd + 1) * DV].astype(jnp.bfloat16)
            v_ext = jnp.concatenate([v_c, ones_blk], axis=1)
            b_col = p_col[rows, N_HEADS + hd:N_HEADS + hd + 1]
            r_col = p_col[rows, hd:hd + 1]
            r_row = p_row[hd:hd + 1, rows]
            b_last = p_row[N_HEADS + hd:N_HEADS + hd + 1,
                           (c + 1) * CHUNK - 1:(c + 1) * CHUNK]

            d = jnp.where(tri, b_col + r_row, -jnp.inf)
            inter = b_col + m
            m_comb = jnp.maximum(inter, jnp.max(d, axis=1, keepdims=True))
            dw = jnp.exp(d - m_comb)
            inter_w = jnp.exp(inter - m_comb)
            s = lax.dot_general(q_c, k_f.astype(jnp.bfloat16),
                                (((1,), (1,)), ((), ())),
                                preferred_element_type=jnp.float32)
            sw = (s * dw).astype(jnp.bfloat16)
            qs = (q_f * inter_w).astype(jnp.bfloat16)
            tot = _dot(jnp.concatenate([sw, qs], axis=1),
                       jnp.concatenate([v_ext, cmat.astype(jnp.bfloat16)], axis=0))
            den = jnp.maximum(jnp.abs(tot[:, DV:]), jnp.exp(-m_comb))
            inv = 1.0 / den
            hout = tot[:, :DV] * jnp.concatenate([inv, inv], axis=1)

            mx = jnp.maximum(m, jnp.max(r_row, axis=1, keepdims=True))
            decay = jnp.exp(m - mx)
            kw = (k_f * jnp.exp(r_col - mx)).astype(jnp.bfloat16)
            upd = lax.dot_general(kw, v_ext, (((0,), (0,)), ((), ())),
                                  preferred_element_type=jnp.float32)
            cmat = decay * cmat + upd
            m = b_last + mx

            hn = hout * lax.rsqrt(jnp.mean(hout * hout, axis=-1, keepdims=True) + EPS)
            hn = hn * ghead_ref[:, hd * DV:(hd + 1) * DV]
            o_c = oz_ref[rows, hd * DV:(hd + 1) * DV]
            z_c = oz_ref[rows, D_ML + hd * DV:D_ML + (hd + 1) * DV]
            y_m = z_c * _sigmoid(z_c) * _sigmoid(o_c) * hn
            mix_ref[rows, hd * DV:(hd + 1) * DV] = y_m.astype(jnp.bfloat16)
        c_ref[hd] = cmat
        m_ref[hd] = jnp.broadcast_to(m, (8, LANES))

    out = _dot(mix_ref[...], wout_ref[...])
    _prenorm_and_qkv(xn_ref, gpre_ref, wmain_ref, h_ref, qkv_ref)
    o_ref[0] = xk_ref[...] + _rms(out, gpost_ref[...])
    xk_ref[...] = xn_ref[0]


def _full(shape):
    return pl.BlockSpec(shape, lambda s: (0,) * len(shape))


def _layer(x, g_pre, g_post, w_in, b_i, b_f, g_head, conv_w, g_conv, w_out, *, ts):
    bsz, seq, d = x.shape
    assert seq % ts == 0 and ts % CHUNK == 0
    tiles_per_seq = seq // ts
    n_tiles = bsz * tiles_per_seq
    gate0 = _U0
    w_main = jnp.concatenate([w_in[:, :gate0], w_in[:, gate0 + 2 * N_HEADS:]],
                             axis=1).astype(jnp.bfloat16)
    w_if = jnp.pad(w_in[:, gate0:gate0 + 2 * N_HEADS],
                   ((0, 0), (0, LANES - 2 * N_HEADS))).astype(jnp.bfloat16)
    b_if = jnp.pad(jnp.concatenate([b_i, b_f]), (0, LANES - 2 * N_HEADS))[None, :]
    grp = jnp.arange(D_CV) // (D_CV // N_GROUPS)
    rmat = (grp[:, None] == jnp.arange(LANES)[None, :]).astype(jnp.float32)
    emat = rmat.T.astype(jnp.bfloat16)
    emat2 = jnp.concatenate([emat, emat], axis=0)
    rmat = (rmat / (D_CV // N_GROUPS)).astype(jnp.bfloat16)
    n_in = w_main.shape[1]
    xt = x.reshape(n_tiles, ts, d)

    kern = functools.partial(_layer_kernel, ts=ts, tiles_per_seq=tiles_per_seq)
    out = pl.pallas_call(
        kern,
        grid=(n_tiles + 1,),
        in_specs=[
            pl.BlockSpec((1, ts, d), lambda s: (jnp.minimum(s, n_tiles - 1), 0, 0)),
            _full((1, d)), _full((1, d)),
            _full((d, n_in)), _full((d, LANES)), _full((1, LANES)),
            _full((1, D_ML)), _full((CONV_W, D_CV)), _full((1, D_CV)),
            _full((D_ML + D_CV, d)), _full((D_CV, LANES)), _full((2 * LANES, D_CV)),
        ],
        out_specs=pl.BlockSpec((1, ts, d), lambda s: (jnp.maximum(s - 1, 0), 0, 0)),
        out_shape=jax.ShapeDtypeStruct(xt.shape, x.dtype),
        scratch_shapes=[
            pltpu.VMEM((N_HEADS, DQK, DV_EXT), jnp.float32),
            pltpu.VMEM((N_HEADS, 8, LANES), jnp.float32),
            pltpu.VMEM((ts + 8, D_CV), jnp.float32),
            pltpu.VMEM((ts, D_ML + D_CV), jnp.bfloat16),
            pltpu.VMEM((ts, 4 * D_CV), jnp.float32),
            pltpu.VMEM((ts, 2 * D_ML), jnp.float32),
            pltpu.VMEM((ts, d), jnp.bfloat16),
            pltpu.VMEM((ts, 2 * DQK * N_HEADS + D_ML), jnp.float32),
            pltpu.VMEM((ts, d), jnp.float32),
        ],
        compiler_params=pltpu.CompilerParams(
            dimension_semantics=("arbitrary",),
            vmem_limit_bytes=VMEM_LIMIT),
        name="hybrid_layer",
    )(xt, g_pre[None, :], g_post[None, :], w_main, w_if, b_if, g_head[None, :],
      conv_w, g_conv[None, :], w_out.astype(jnp.bfloat16), rmat, emat2)
    return out.reshape(x.shape)


def kernel(x, norm_pre, norm_post, w_in, b_igate, b_fgate, head_norm, conv_w, conv_norm, w_out):
    ts = min(SEQ_TILE, x.shape[1])
    for l in range(norm_pre.shape[0]):
        x = _layer(x, norm_pre[l], norm_post[l], w_in[l], b_igate[l], b_fgate[l],
                   head_norm[l], conv_w[l], conv_norm[l], w_out[l], ts=ts)
    return x
```
